```python
import math
import jax, jax.numpy as jnp
from jax import lax
import numpy as np

D_MODEL = 1024
BATCH = 8
SEQ = 2048
DEPTH = 2

R_HEADS = 4
R_DK = 64
R_DV = 128
R_CHUNK = 128
ROPE_BASE = 10000.0
G_HEADS = 4
G_DK = 128
G_DV = 128
G_CHUNK = 64
CONV_K = 4
CONV_CH = 2 * G_HEADS * G_DK + G_HEADS * G_DV
IN_SIZES = (R_HEADS * R_DK, R_HEADS * R_DK, R_HEADS * R_DV, R_HEADS * R_DV,
            CONV_CH, G_HEADS * G_DV, G_HEADS, G_HEADS)
IN_COLS = 3592
MIX_WIDTH = R_HEADS * R_DV + G_HEADS * G_DV
D_FF = 2816
N_EXPERTS = 8
TOP_K = 2
D_FF_EXPERT = 3584
N_DENSE = (DEPTH + 1) // 2
N_MOE = DEPTH // 2
ALPHA = (2 * DEPTH) ** 0.25
BETA_INIT = (8 * DEPTH) ** -0.25
LN_EPS = 1e-5
NORM_EPS = 1e-6

kernel_name = 'hybrid_retnet_gdn_deepnorm_moe'


def layer_norm(x, g, b):
    xf = x.astype(jnp.float32)
    mu = jnp.mean(xf, axis=-1, keepdims=True)
    var = jnp.mean(jnp.square(xf - mu), axis=-1, keepdims=True)
    y = (xf - mu) * lax.rsqrt(var + LN_EPS) * g.astype(jnp.float32) + b.astype(jnp.float32)
    return y.astype(x.dtype)


def to_chunks(t, c):
    b, tl = t.shape[:2]
    t = t.reshape(b, tl // c, c, *t.shape[2:])
    return jnp.moveaxis(t, 2, 3)


def from_chunks(t):
    t = jnp.moveaxis(t, 3, 2)
    return t.reshape(t.shape[0], -1, *t.shape[3:])


def rope(t, pos):
    d = t.shape[-1]
    inv = ROPE_BASE ** (-jnp.arange(0, d, 2, dtype=jnp.float32) / d)
    ang = pos[:, None] * inv[None, :]
    cos = jnp.cos(ang)[:, None, :]
    sin = jnp.sin(ang)[:, None, :]
    t1, t2 = t[..., : d // 2], t[..., d // 2:]
    return jnp.concatenate([t1 * cos - t2 * sin, t1 * sin + t2 * cos], axis=-1)


def retention(q, k, v):
    b, tl, h, dk = q.shape
    dv = v.shape[-1]
    c = R_CHUNK
    log_g = jnp.log(1.0 - 2.0 ** (-5.0 - jnp.arange(h, dtype=jnp.float32)))
    idx = jnp.arange(c, dtype=jnp.float32)
    causal = jnp.tril(jnp.ones((c, c), dtype=bool))
    dmat = jnp.exp(jnp.where(causal, (idx[:, None] - idx[None, :])[None] * log_g[:, None, None], -jnp.inf))
    qc, kc, vc = to_chunks(q, c), to_chunks(k, c), to_chunks(v, c)
    scores = jnp.einsum('bnhid,bnhjd->bnhij', qc, kc) * dmat
    o_intra = jnp.einsum('bnhij,bnhje->bnhie', scores, vc)
    zeta = jnp.exp((c - 1.0 - idx)[None, :] * log_g[:, None])
    xi = jnp.exp((idx + 1.0)[None, :] * log_g[:, None])
    kv = jnp.einsum('bnhcd,bnhce->nbhde', kc * zeta[:, :, None], vc)
    chunk_decay = jnp.exp(c * log_g)[:, None, None]

    def step(s, kv_n):
        return s * chunk_decay + kv_n, s

    _, s_prev = lax.scan(step, jnp.zeros((b, h, dk, dv), jnp.float32), kv)
    o_inter = jnp.einsum('bnhcd,nbhde->bnhce', qc * xi[:, :, None], s_prev)
    return from_chunks(o_intra + o_inter)


def gated_delta_rule(q, k, v, g, beta):
    b, tl, h, dk = q.shape
    dv = v.shape[-1]
    c = G_CHUNK
    qc, kc, vc = to_chunks(q, c), to_chunks(k, c), to_chunks(v, c)
    gc = jnp.cumsum(to_chunks(g, c), axis=-1)
    bc = to_chunks(beta, c)[..., None]
    causal = jnp.tril(jnp.ones((c, c), dtype=bool))
    strict = jnp.tril(jnp.ones((c, c), dtype=bool), -1)
    decay = jnp.exp(jnp.where(causal, gc[..., :, None] - gc[..., None, :], -jnp.inf))
    kb = kc * bc
    a = jnp.where(strict, jnp.einsum('bnhid,bnhjd->bnhij', kb, kc) * decay, 0.0)
    rhs = jnp.concatenate([vc * bc, kb * jnp.exp(gc)[..., None]], axis=-1)
    sol = lax.linalg.triangular_solve(a, rhs, left_side=True, lower=True, unit_diagonal=True)
    u, w = sol[..., :dv], sol[..., dv:]
    qk = jnp.where(causal, jnp.einsum('bnhid,bnhjd->bnhij', qc, kc) * decay, 0.0)
    q_dec = qc * jnp.exp(gc)[..., None]
    k_dec = kc * jnp.exp(gc[..., -1:] - gc)[..., None]
    cd = jnp.exp(gc[..., -1])[..., None, None]
    xs = (jnp.moveaxis(u, 1, 0), jnp.moveaxis(w, 1, 0), jnp.moveaxis(qk, 1, 0),
          jnp.moveaxis(q_dec, 1, 0), jnp.moveaxis(k_dec, 1, 0), jnp.moveaxis(cd, 1, 0))

    def step(s, xs_n):
        u_n, w_n, qk_n, qd_n, kd_n, cd_n = xs_n
        v_new = u_n - jnp.einsum('bhcd,bhde->bhce', w_n, s)
        o_n = jnp.einsum('bhcd,bhde->bhce', qd_n, s) + jnp.einsum('bhij,bhje->bhie', qk_n, v_new)
        s = s * cd_n + jnp.einsum('bhcd,bhce->bhde', kd_n, v_new)
        return s, o_n

    _, o = lax.scan(step, jnp.zeros((b, h, dk, dv), jnp.float32), xs)
    return from_chunks(jnp.moveaxis(o, 0, 1))


def causal_conv(x, w):
    return lax.conv_general_dilated(x, w[:, None, :], window_strides=(1,), padding=((CONV_K - 1, 0),),
                                    dimension_numbers=('NWC', 'WIO', 'NWC'),
                                    feature_group_count=x.shape[-1])


def l2norm(t):
    return t * lax.rsqrt(jnp.sum(jnp.square(t), axis=-1, keepdims=True) + NORM_EPS)


def hybrid_mixer(x, w_in, conv_w, a_log, dt_bias, gdn_norm_w, w_o):
    b, tl, _ = x.shape
    f32 = jnp.float32
    hproj = x @ w_in
    splits = np.cumsum(IN_SIZES)[:-1].tolist()
    rq, rk, rv, rg, gqkv, gg, ga, gb = jnp.split(hproj, splits, axis=-1)
    pos = jnp.arange(tl, dtype=f32)
    rq = rope(rq.reshape(b, tl, R_HEADS, R_DK).astype(f32), pos)
    rk = rope(rk.reshape(b, tl, R_HEADS, R_DK).astype(f32), pos) * (R_DK ** -0.5)
    rv = rv.reshape(b, tl, R_HEADS, R_DV).astype(f32)
    ro = retention(rq, rk, rv)
    mu = jnp.mean(ro, axis=-1, keepdims=True)
    var = jnp.mean(jnp.square(ro - mu), axis=-1, keepdims=True)
    ro = (ro - mu) * lax.rsqrt(var + LN_EPS) * jax.nn.silu(rg.reshape(b, tl, R_HEADS, R_DV).astype(f32))
    gqkv = jax.nn.silu(causal_conv(gqkv, conv_w)).astype(f32)
    gq, gk, gv = jnp.split(gqkv, [G_HEADS * G_DK, 2 * G_HEADS * G_DK], axis=-1)
    gq = l2norm(gq.reshape(b, tl, G_HEADS, G_DK)) * (G_DK ** -0.5)
    gk = l2norm(gk.reshape(b, tl, G_HEADS, G_DK))
    gv = gv.reshape(b, tl, G_HEADS, G_DV)
    g = -jnp.exp(a_log.astype(f32)) * jax.nn.softplus(ga.astype(f32) + dt_bias.astype(f32))
    beta = jax.nn.sigmoid(gb.astype(f32))
    go = gated_delta_rule(gq, gk, gv, g, beta)
    go = go * lax.rsqrt(jnp.mean(jnp.square(go), axis=-1, keepdims=True) + NORM_EPS)
    go = go * gdn_norm_w.astype(f32) * jax.nn.silu(gg.reshape(b, tl, G_HEADS, G_DV).astype(f32))
    o = jnp.concatenate([ro.reshape(b, tl, -1), go.reshape(b, tl, -1)], axis=-1).astype(x.dtype)
    return o @ w_o


def swiglu(x, w_gate, w_up, w_down):
    return (jax.nn.silu(x @ w_gate) * (x @ w_up)) @ w_down


def moe_swiglu(x, router_w, w_gate, w_up, w_down):
    logits = (x @ router_w).astype(jnp.float32)
    top_vals, top_idx = lax.top_k(logits, TOP_K)
    gates = jax.nn.softmax(top_vals, axis=-1)
    combine = jnp.sum(jax.nn.one_hot(top_idx, N_EXPERTS, dtype=jnp.float32) * gates[..., None], axis=-2)
    combine = combine.astype(x.dtype)
    y = jnp.zeros_like(x)
    for e in range(N_EXPERTS):
        y = y + combine[..., e:e + 1] * swiglu(x, w_gate[e], w_up[e], w_down[e])
    return y


def setup_inputs(seed: int = 0) -> dict:
    key = jax.random.key(seed)
    ks = jax.random.split(key, 18)
    f32 = jnp.float32

    def nrm(k, shape, scale):
        return jax.random.normal(k, shape, f32) * scale

    x = nrm(ks[0], (BATCH, SEQ, D_MODEL), 1.0)
    w_in = nrm(ks[1], (DEPTH, D_MODEL, IN_COLS), D_MODEL ** -0.5)
    conv_w = nrm(ks[2], (DEPTH, CONV_K, CONV_CH), CONV_K ** -0.5)
    a_log = jnp.log(jax.random.uniform(ks[3], (DEPTH, G_HEADS), f32, 1.0, 16.0))
    dt = jnp.exp(jax.random.uniform(ks[4], (DEPTH, G_HEADS), f32, math.log(1e-3), math.log(1e-1)))
    dt_bias = dt + jnp.log(-jnp.expm1(-dt))
    gdn_norm_w = 1.0 + nrm(ks[5], (DEPTH, G_DV), 0.02)
    w_o = nrm(ks[6], (DEPTH, MIX_WIDTH, D_MODEL), BETA_INIT * MIX_WIDTH ** -0.5)
    ln1_g = 1.0 + nrm(ks[7], (DEPTH, D_MODEL), 0.02)
    ln1_b = nrm(ks[8], (DEPTH, D_MODEL), 0.02)
    ln2_g = 1.0 + nrm(ks[9], (DEPTH, D_MODEL), 0.02)
    ln2_b = nrm(ks[10], (DEPTH, D_MODEL), 0.02)
    ffn_w_gate = nrm(ks[11], (N_DENSE, D_MODEL, D_FF), D_MODEL ** -0.5)
    ffn_w_up = nrm(ks[12], (N_DENSE, D_MODEL, D_FF), D_MODEL ** -0.5)
    ffn_w_down = nrm(ks[13], (N_DENSE, D_FF, D_MODEL), BETA_INIT * D_FF ** -0.5)
    router_w = nrm(ks[14], (N_MOE, D_MODEL, N_EXPERTS), D_MODEL ** -0.5)
    moe_w_gate = nrm(ks[15], (N_MOE, N_EXPERTS, D_MODEL, D_FF_EXPERT), D_MODEL ** -0.5)
    moe_w_up = nrm(ks[16], (N_MOE, N_EXPERTS, D_MODEL, D_FF_EXPERT), D_MODEL ** -0.5)
    moe_w_down = nrm(ks[17], (N_MOE, N_EXPERTS, D_FF_EXPERT, D_MODEL), BETA_INIT * D_FF_EXPERT ** -0.5)
    return {'x': x, 'w_in': w_in, 'conv_w': conv_w, 'a_log': a_log, 'dt_bias': dt_bias,
            'gdn_norm_w': gdn_norm_w, 'w_o': w_o, 'ln1_g': ln1_g, 'ln1_b': ln1_b,
            'ln2_g': ln2_g, 'ln2_b': ln2_b, 'ffn_w_gate': ffn_w_gate, 'ffn_w_up': ffn_w_up,
            'ffn_w_down': ffn_w_down, 'router_w': router_w, 'moe_w_gate': moe_w_gate,
            'moe_w_up': moe_w_up, 'moe_w_down': moe_w_down}


def reference(x, w_in, conv_w, a_log, dt_bias, gdn_norm_w, w_o, ln1_g, ln1_b, ln2_g, ln2_b,
              ffn_w_gate, ffn_w_up, ffn_w_down, router_w, moe_w_gate, moe_w_up, moe_w_down):
    for l in range(DEPTH):
        mix = hybrid_mixer(x, w_in[l], conv_w[l], a_log[l], dt_bias[l], gdn_norm_w[l], w_o[l])
        x = layer_norm(ALPHA * x + mix, ln1_g[l], ln1_b[l])
        if l % 2 == 0:
            f = swiglu(x, ffn_w_gate[l // 2], ffn_w_up[l // 2], ffn_w_down[l // 2])
        else:
            f = moe_swiglu(x, router_w[l // 2], moe_w_gate[l // 2], moe_w_up[l // 2], moe_w_down[l // 2])
        x = layer_norm(ALPHA * x + f, ln2_g[l], ln2_b[l])
    return x
```

```python
import functools
import math

import numpy as np
import jax
import jax.numpy as jnp
from jax import lax
from jax.experimental import pallas as pl
from jax.experimental.pallas import tpu as pltpu

F32 = jnp.float32
BF16 = jnp.bfloat16
HIGHEST = lax.Precision.HIGHEST

D_MODEL = 1024
DEPTH = 2
R_HEADS = 4
R_DK = 64
R_DV = 128
R_CHUNK = 128
ROPE_BASE = 10000.0
G_HEADS = 4
G_DK = 128
G_DV = 128
G_CHUNK = 64
CONV_K = 4
CONV_CH = 2 * G_HEADS * G_DK + G_HEADS * G_DV
IN_COLS = 3592
LANE = 128
IN_COLS_PAD = 3712
D_FF = 2816
N_EXPERTS = 8
D_FF_EXPERT = 3584
ALPHA = (2 * DEPTH) ** 0.25
LN_EPS = 1e-5
NORM_EPS = 1e-6

COL_RQ = 0
COL_RK = 256
COL_RV = 512
COL_RG = 1024
COL_GQKV = 1536
COL_GG = 3072
COL_GAB = 3584

VMEM_LIMIT = 56 * 1024 * 1024


def _cparams(sem):
    return pltpu.CompilerParams(dimension_semantics=sem, vmem_limit_bytes=VMEM_LIMIT)


def _silu(x):
    return x / (1.0 + jnp.exp(-x))


def _dot(a, b):
    return jnp.dot(a.astype(BF16), b.astype(BF16), preferred_element_type=F32)


def _dot_nt(a, b):
    return lax.dot_general(a.astype(BF16), b.astype(BF16), (((1,), (1,)), ((), ())),
                           preferred_element_type=F32)


def _dot_hi(a, b):
    return jnp.dot(a, b, preferred_element_type=F32, precision=HIGHEST)


def _inproj_kernel(x_ref, w_ref, o_ref):
    xb = x_ref[...].astype(BF16)
    ncols = o_ref.shape[1]
    c = 0
    while c < ncols:
        cw = min(512, ncols - c)
        o_ref[:, c:c + cw] = jnp.dot(xb, w_ref[:, c:c + cw], preferred_element_type=F32)
        c += cw


def _inproj(x2, w_pad, tm=512):
    n = x2.shape[0]
    return pl.pallas_call(
        _inproj_kernel,
        grid=(n // tm,),
        in_specs=[pl.BlockSpec((tm, D_MODEL), lambda i: (i, 0)),
                  pl.BlockSpec((D_MODEL, IN_COLS_PAD), lambda i: (0, 0))],
        out_specs=pl.BlockSpec((tm, IN_COLS_PAD), lambda i: (i, 0)),
        out_shape=jax.ShapeDtypeStruct((n, IN_COLS_PAD), F32),
        compiler_params=_cparams(("parallel",)),
        name="inproj",
    )(x2, w_pad)


def _retention_kernel(q_ref, k_ref, v_ref, g_ref, cos_ref, sin_ref, dmat_ref, zeta_ref, xi_ref,
                      cd_ref, o_ref, s_ref):
    @pl.when(pl.program_id(1) == 0)
    def _():
        s_ref[...] = jnp.zeros_like(s_ref)

    cos = cos_ref[...]
    sin = sin_ref[...]

    def rope(t):
        t1, t2 = t[:, :LANE], t[:, LANE:]
        return jnp.concatenate([t1 * cos - t2 * sin, t1 * sin + t2 * cos], axis=1)

    qr = rope(q_ref[...])
    kr = rope(k_ref[...]) * (R_DK ** -0.5)
    v = v_ref[...]
    g = g_ref[...]
    s_prev = s_ref[...]
    qx = qr * xi_ref[...]
    kz = kr * zeta_ref[...]
    krb = kr.astype(BF16)
    vb = v.astype(BF16)
    lane = lax.broadcasted_iota(jnp.int32, qr.shape, 1)
    head_of_lane = (lane % LANE) // (R_DK // 2)
    for h in range(R_HEADS):
        mh = head_of_lane == h
        sl = slice(h * R_DV, (h + 1) * R_DV)
        scores = _dot_nt(jnp.where(mh, qr, 0.0), krb) * dmat_ref[h]
        o = _dot(scores, vb[:, sl]) + _dot(jnp.where(mh, qx, 0.0), s_prev[:, sl])
        mu = jnp.mean(o, axis=-1, keepdims=True)
        d = o - mu
        var = jnp.mean(d * d, axis=-1, keepdims=True)
        o_ref[:, sl] = d * lax.rsqrt(var + LN_EPS) * _silu(g[:, sl])
    s_ref[...] = s_prev * cd_ref[...] + _dot(kz.T, vb)


def _retention_tables(t_len):
    c = R_CHUNK
    half = R_DK // 2
    inv = ROPE_BASE ** (-jnp.arange(0, R_DK, 2, dtype=F32) / R_DK)
    pos = jnp.arange(t_len, dtype=F32)
    ang = pos[:, None] * inv[None, :]
    cos = jnp.tile(jnp.cos(ang), (1, R_HEADS))
    sin = jnp.tile(jnp.sin(ang), (1, R_HEADS))
    log_g = jnp.log(1.0 - 2.0 ** (-5.0 - jnp.arange(R_HEADS, dtype=F32)))
    idx = jnp.arange(c, dtype=F32)
    causal = jnp.tril(jnp.ones((c, c), dtype=bool))
    dmat = jnp.exp(jnp.where(causal, (idx[:, None] - idx[None, :])[None] * log_g[:, None, None], -jnp.inf))
    zeta = jnp.exp((c - 1.0 - idx)[None, :] * log_g[:, None])
    xi = jnp.exp((idx + 1.0)[None, :] * log_g[:, None])
    lane_head = (jnp.arange(2 * LANE) % LANE) // half
    zeta_t = zeta.T[:, lane_head]
    xi_t = xi.T[:, lane_head]
    cd = jnp.exp(c * log_g)[lane_head]
    cd_t = jnp.broadcast_to(cd[:, None], (2 * LANE, R_HEADS * R_DV))
    return cos, sin, dmat, zeta_t, xi_t, cd_t


def _retention(hproj, batch, t_len):
    c = R_CHUNK
    nch = t_len // c
    cos, sin, dmat, zeta_t, xi_t, cd_t = _retention_tables(t_len)
    row = lambda b, n: b * nch + n
    const2 = lambda b, n: (0, 0)
    return pl.pallas_call(
        _retention_kernel,
        grid=(batch, nch),
        in_specs=[pl.BlockSpec((c, 256), lambda b, n: (row(b, n), COL_RQ // 256)),
                  pl.BlockSpec((c, 256), lambda b, n: (row(b, n), COL_RK // 256)),
                  pl.BlockSpec((c, 512), lambda b, n: (row(b, n), COL_RV // 512)),
                  pl.BlockSpec((c, 512), lambda b, n: (row(b, n), COL_RG // 512)),
                  pl.BlockSpec((c, LANE), lambda b, n: (n, 0)),
                  pl.BlockSpec((c, LANE), lambda b, n: (n, 0)),
                  pl.BlockSpec((R_HEADS, c, c), lambda b, n: (0, 0, 0)),
                  pl.BlockSpec((c, 256), const2),
                  pl.BlockSpec((c, 256), const2),
                  pl.BlockSpec((256, R_HEADS * R_DV), const2)],
        out_specs=pl.BlockSpec((c, R_HEADS * R_DV), lambda b, n: (row(b, n), 0)),
        out_shape=jax.ShapeDtypeStruct((batch * t_len, R_HEADS * R_DV), F32),
        scratch_shapes=[pltpu.VMEM((256, R_HEADS * R_DV), F32)],
        compiler_params=_cparams(("parallel", "arbitrary")),
        name="retention",
    )(hproj, hproj, hproj, hproj, cos, sin, dmat, zeta_t, xi_t, cd_t)


def _gdn_kernel(qkv_ref, gg_ref, gab_ref, convw_ref, alog_ref, dtb_ref, nw_ref, o_ref,
                s_ref, carry_ref):
    c = G_CHUNK

    @pl.when(pl.program_id(1) == 0)
    def _():
        s_ref[...] = jnp.zeros_like(s_ref)
        carry_ref[...] = jnp.zeros_like(carry_ref)

    x = qkv_ref[...]
    prev8 = carry_ref[...]
    row8 = lax.broadcasted_iota(jnp.int32, prev8.shape, 0)
    acc = x * convw_ref[CONV_K - 1:CONV_K, :]
    for j in range(1, CONV_K):
        rolled = pltpu.roll(x, j, 0)
        head8 = jnp.where(row8 < j, pltpu.roll(prev8, j, 0), rolled[:8])
        shifted = jnp.concatenate([head8, rolled[8:]], axis=0)
        acc = acc + shifted * convw_ref[CONV_K - 1 - j:CONV_K - j, :]
    carry_ref[...] = x[c - 8:, :]
    qkv = _silu(acc)

    gab = gab_ref[...]
    z = gab + dtb_ref[...]
    softplus = jnp.maximum(z, 0.0) + jnp.log1p(jnp.exp(-jnp.abs(z)))
    g = -jnp.exp(alog_ref[...]) * softplus
    beta = 1.0 / (1.0 + jnp.exp(-gab))
    ri = lax.broadcasted_iota(jnp.int32, (c, c), 0)
    ci = lax.broadcasted_iota(jnp.int32, (c, c), 1)
    causal = ri >= ci
    strict = ri > ci
    gc = _dot_hi(causal.astype(F32), g)
    gc_t = gc.T
    eye = (ri == ci).astype(F32)
    nw = nw_ref[...]
    gg = gg_ref[...]

    for h in range(G_HEADS):
        sl = slice(h * G_DK, (h + 1) * G_DK)
        q = qkv[:, h * G_DK:(h + 1) * G_DK]
        k = qkv[:, G_HEADS * G_DK + h * G_DK:G_HEADS * G_DK + (h + 1) * G_DK]
        v = qkv[:, 2 * G_HEADS * G_DK + h * G_DV:2 * G_HEADS * G_DK + (h + 1) * G_DV]
        q = q * lax.rsqrt(jnp.sum(q * q, axis=-1, keepdims=True) + NORM_EPS) * (G_DK ** -0.5)
        k = k * lax.rsqrt(jnp.sum(k * k, axis=-1, keepdims=True) + NORM_EPS)
        gcol = gc[:, h:h + 1]
        grow = gc_t[h:h + 1, :]
        bcol = beta[:, G_HEADS + h:G_HEADS + h + 1]
        decay = jnp.exp(jnp.where(causal, gcol - grow, -jnp.inf))
        egc = jnp.exp(gcol)
        kb = k * bcol
        a = jnp.where(strict, _dot_nt(kb, k) * decay, 0.0)
        inv = eye - a
        p = a
        for _ in range(int(math.log2(c)) - 1):
            p = _dot_hi(p, p)
            inv = inv + _dot_hi(inv, p)
        rhs = jnp.concatenate([v * bcol, kb * egc], axis=1)
        sol = _dot_hi(inv, rhs)
        u, w = sol[:, :G_DV], sol[:, G_DV:]
        qk = jnp.where(causal, _dot_nt(q, k) * decay, 0.0)
        q_dec = q * egc
        glast = gcol[c - 1:c, :]
        k_dec = k * jnp.exp(glast - gcol)
        s = s_ref[h]
        wq = _dot(jnp.concatenate([w, q_dec], axis=0), s)
        v_new = u - wq[:c]
        o = wq[c:] + _dot(qk, v_new)
        s_ref[h] = s * jnp.exp(glast) + _dot(k_dec.T, v_new)
        o = o * lax.rsqrt(jnp.mean(o * o, axis=-1, keepdims=True) + NORM_EPS)
        o_ref[:, sl] = o * nw * _silu(gg[:, sl])


def _lane_pad(vec):
    return jnp.zeros((1, LANE), F32).at[0, :vec.shape[0]].set(vec.astype(F32))


def _gdn(hproj, conv_w, a_log, dt_bias, gdn_norm_w, batch, t_len):
    c = G_CHUNK
    nch = t_len // c
    convw = jnp.zeros((8, CONV_CH), F32).at[:CONV_K].set(conv_w.astype(F32))
    alog = _lane_pad(a_log)
    dtb = _lane_pad(dt_bias)
    nw = gdn_norm_w.astype(F32).reshape(1, G_DV)
    row = lambda b, n: b * nch + n
    const2 = lambda b, n: (0, 0)
    return pl.pallas_call(
        _gdn_kernel,
        grid=(batch, nch),
        in_specs=[pl.BlockSpec((c, CONV_CH), lambda b, n: (row(b, n), COL_GQKV // CONV_CH)),
                  pl.BlockSpec((c, 512), lambda b, n: (row(b, n), COL_GG // 512)),
                  pl.BlockSpec((c, LANE), lambda b, n: (row(b, n), COL_GAB // LANE)),
                  pl.BlockSpec((8, CONV_CH), const2),
                  pl.BlockSpec((1, LANE), const2),
                  pl.BlockSpec((1, LANE), const2),
                  pl.BlockSpec((1, G_DV), const2)],
        out_specs=pl.BlockSpec((c, G_HEADS * G_DV), lambda b, n: (row(b, n), 0)),
        out_shape=jax.ShapeDtypeStruct((batch * t_len, G_HEADS * G_DV), F32),
        scratch_shapes=[pltpu.VMEM((G_HEADS, G_DK, G_DV), F32),
                        pltpu.VMEM((8, CONV_CH), F32)],
        compiler_params=_cparams(("parallel", "arbitrary")),
        name="gdn",
    )(hproj, hproj, hproj, convw, alog, dtb, nw)


def _layer_norm(y, g, b):
    mu = jnp.mean(y, axis=-1, keepdims=True)
    d = y - mu
    var = jnp.mean(d * d, axis=-1, keepdims=True)
    return d * lax.rsqrt(var + LN_EPS) * g + b


def _outproj_ln_kernel(ro_ref, go_ref, x_ref, w_ref, g_ref, b_ref, o_ref):
    half = ro_ref.shape[1]
    mix = _dot(ro_ref[...], w_ref[:half, :]) + _dot(go_ref[...], w_ref[half:, :])
    o_ref[...] = _layer_norm(ALPHA * x_ref[...] + mix, g_ref[...], b_ref[...])


def _outproj_ln(ro, go, x2, w_o, ln_g, ln_b, tm=512):
    n = x2.shape[0]
    half = ro.shape[1]
    return pl.pallas_call(
        _outproj_ln_kernel,
        grid=(n // tm,),
        in_specs=[pl.BlockSpec((tm, half), lambda i: (i, 0)),
                  pl.BlockSpec((tm, half), lambda i: (i, 0)),
                  pl.BlockSpec((tm, D_MODEL), lambda i: (i, 0)),
                  pl.BlockSpec((2 * half, D_MODEL), lambda i: (0, 0)),
                  pl.BlockSpec((1, D_MODEL), lambda i: (0, 0)),
                  pl.BlockSpec((1, D_MODEL), lambda i: (0, 0))],
        out_specs=pl.BlockSpec((tm, D_MODEL), lambda i: (i, 0)),
        out_shape=jax.ShapeDtypeStruct((n, D_MODEL), F32),
        compiler_params=_cparams(("parallel",)),
        name="outproj_ln",
    )(ro, go, x2, w_o.astype(BF16), ln_g.reshape(1, -1).astype(F32), ln_b.reshape(1, -1).astype(F32))


def _router_kernel(x_ref, w_ref, o_ref):
    logits = _dot_hi(x_ref[...], w_ref[...])
    lane = lax.broadcasted_iota(jnp.int32, logits.shape, 1)
    neg = jnp.float32(-jnp.inf)
    lg = jnp.where(lane < N_EXPERTS, logits, neg)
    m1 = jnp.max(lg, axis=-1, keepdims=True)
    i1 = jnp.min(jnp.where(lg == m1, lane, LANE), axis=-1, keepdims=True)
    lg2 = jnp.where(lane == i1, neg, lg)
    m2 = jnp.max(lg2, axis=-1, keepdims=True)
    i2 = jnp.min(jnp.where(lg2 == m2, lane, LANE), axis=-1, keepdims=True)
    e2 = jnp.exp(m2 - m1)
    den = 1.0 + e2
    o_ref[...] = jnp.where(lane == i1, 1.0 / den, 0.0) + jnp.where(lane == i2, e2 / den, 0.0)


def _router(x2, router_w, tm=512):
    n = x2.shape[0]
    w = jnp.zeros((D_MODEL, LANE), F32).at[:, :N_EXPERTS].set(router_w.astype(F32))
    return pl.pallas_call(
        _router_kernel,
        grid=(n // tm,),
        in_specs=[pl.BlockSpec((tm, D_MODEL), lambda i: (i, 0)),
                  pl.BlockSpec((D_MODEL, LANE), lambda i: (0, 0))],
        out_specs=pl.BlockSpec((tm, LANE), lambda i: (i, 0)),
        out_shape=jax.ShapeDtypeStruct((n, LANE), F32),
        compiler_params=_cparams(("parallel",)),
        name="router",
    )(x2, w)


def _ffn_ln_kernel(x_ref, comb_ref, wg_ref, wu_ref, wd_ref, g_ref, b_ref, o_ref, acc_ref, *,
                   weighted):
    e = pl.program_id(1)
    f = pl.program_id(2)

    @pl.when((e == 0) & (f == 0))
    def _():
        acc_ref[...] = jnp.zeros_like(acc_ref)

    xb = x_ref[...].astype(BF16)
    hg = jnp.dot(xb, wg_ref[0], preferred_element_type=F32)
    hu = jnp.dot(xb, wu_ref[0], preferred_element_type=F32)
    hmid = _silu(hg) * hu
    if weighted:
        comb = comb_ref[...]
        lane = lax.broadcasted_iota(jnp.int32, comb.shape, 1)
        hmid = hmid * jnp.sum(jnp.where(lane == e, comb, 0.0), axis=-1, keepdims=True)
    acc_ref[...] += jnp.dot(hmid.astype(BF16), wd_ref[0], preferred_element_type=F32)

    @pl.when((e == pl.num_programs(1) - 1) & (f == pl.num_programs(2) - 1))
    def _():
        o_ref[...] = _layer_norm(ALPHA * x_ref[...] + acc_ref[...], g_ref[...], b_ref[...])


def _ffn_ln(x2, comb, w_gate, w_up, w_down, ln_g, ln_b, *, weighted, tm, tf):
    n = x2.shape[0]
    ne, _, dff = w_gate.shape
    return pl.pallas_call(
        functools.partial(_ffn_ln_kernel, weighted=weighted),
        grid=(n // tm, ne, dff // tf),
        in_specs=[pl.BlockSpec((tm, D_MODEL), lambda i, e, f: (i, 0)),
                  pl.BlockSpec((tm, LANE), lambda i, e, f: (i, 0)),
                  pl.BlockSpec((1, D_MODEL, tf), lambda i, e, f: (e, 0, f)),
                  pl.BlockSpec((1, D_MODEL, tf), lambda i, e, f: (e, 0, f)),
                  pl.BlockSpec((1, tf, D_MODEL), lambda i, e, f: (e, f, 0)),
                  pl.BlockSpec((1, D_MODEL), lambda i, e, f: (0, 0)),
                  pl.BlockSpec((1, D_MODEL), lambda i, e, f: (0, 0))],
        out_specs=pl.BlockSpec((tm, D_MODEL), lambda i, e, f: (i, 0)),
        out_shape=jax.ShapeDtypeStruct((n, D_MODEL), F32),
        scratch_shapes=[pltpu.VMEM((tm, D_MODEL), F32)],
        compiler_params=_cparams(("parallel", "arbitrary", "arbitrary")),
        name="moe_ln" if weighted else "ffn_ln",
    )(x2, comb, w_gate.astype(BF16), w_up.astype(BF16), w_down.astype(BF16),
      ln_g.reshape(1, -1).astype(F32), ln_b.reshape(1, -1).astype(F32))


def _prep_w_in(w):
    half = R_DK // 2
    lane = np.arange(LANE)
    first = (lane // half) * R_DK + lane % half
    perm = np.concatenate([first, first + half])
    cols = np.arange(IN_COLS)
    cols[COL_RQ:COL_RQ + 256] = COL_RQ + perm
    cols[COL_RK:COL_RK + 256] = COL_RK + perm
    w = w[:, cols]
    return jnp.pad(w, ((0, 0), (0, IN_COLS_PAD - IN_COLS))).astype(BF16)


def kernel(x, w_in, conv_w, a_log, dt_bias, gdn_norm_w, w_o, ln1_g, ln1_b, ln2_g, ln2_b,
           ffn_w_gate, ffn_w_up, ffn_w_down, router_w, moe_w_gate, moe_w_up, moe_w_down):
    batch, t_len, d = x.shape
    x2 = x.reshape(batch * t_len, d).astype(F32)
    for l in range(DEPTH):
        hproj = _inproj(x2, _prep_w_in(w_in[l]))
        ro = _retention(hproj, batch, t_len)
        go = _gdn(hproj, conv_w[l], a_log[l], dt_bias[l], gdn_norm_w[l], batch, t_len)
        x2 = _outproj_ln(ro, go, x2, w_o[l], ln1_g[l], ln1_b[l])
        if l % 2 == 0:
            ones = jnp.ones((x2.shape[0], LANE), F32)
            x2 = _ffn_ln(x2, ones, ffn_w_gate[l // 2][None], ffn_w_up[l // 2][None],
                         ffn_w_down[l // 2][None], ln2_g[l], ln2_b[l], weighted=False,
                         tm=512, tf=1408)
        else:
            comb = _router(x2, router_w[l // 2])
            x2 = _ffn_ln(x2, comb, moe_w_gate[l // 2], moe_w_up[l // 2], moe_w_down[l // 2],
                         ln2_g[l], ln2_b[l], weighted=True, tm=512, tf=896)
    return x2.reshape(batch, t_len, d).astype(x.dtype)
```

```python
import functools
import math

import numpy as np
import jax
import jax.numpy as jnp
from jax import lax
from jax.experimental import pallas as pl
from jax.experimental.pallas import tpu as pltpu

F32 = jnp.float32
BF16 = jnp.bfloat16
HIGHEST = lax.Precision.HIGHEST

D_MODEL = 1024
DEPTH = 2
R_HEADS = 4
R_DK = 64
R_DV = 128
R_CHUNK = 128
ROPE_BASE = 10000.0
G_HEADS = 4
G_DK = 128
G_DV = 128
G_CHUNK = 128
G_BLOCK = 256
CONV_K = 4
CONV_CH = 2 * G_HEADS * G_DK + G_HEADS * G_DV
IN_COLS = 3592
LANE = 128
IN_COLS_PAD = 3712
D_FF = 2816
N_EXPERTS = 8
D_FF_EXPERT = 3584
ALPHA = (2 * DEPTH) ** 0.25
LN_EPS = 1e-5
NORM_EPS = 1e-6

COL_RQ = 0
COL_RK = 256
COL_RV = 512
COL_RG = 1024
COL_GQKV = 1536
COL_GG = 3072
COL_GAB = 3584

VMEM_LIMIT = 56 * 1024 * 1024


def _cparams(sem):
    return pltpu.CompilerParams(dimension_semantics=sem, vmem_limit_bytes=VMEM_LIMIT)


def _silu(x):
    return x / (1.0 + jnp.exp(-x))


def _dot(a, b):
    return jnp.dot(a.astype(BF16), b.astype(BF16), preferred_element_type=F32)


def _dot_nt(a, b):
    return lax.dot_general(a.astype(BF16), b.astype(BF16), (((1,), (1,)), ((), ())),
                           preferred_element_type=F32)


def _dot_hi(a, b):
    return jnp.dot(a, b, preferred_element_type=F32, precision=HIGHEST)


def _inproj_kernel(x_ref, w_ref, o_ref):
    xb = x_ref[...].astype(BF16)
    ncols = o_ref.shape[1]
    c = 0
    while c < ncols:
        cw = min(512, ncols - c)
        o_ref[:, c:c + cw] = jnp.dot(xb, w_ref[:, c:c + cw], preferred_element_type=F32)
        c += cw


def _inproj(x2, w_pad, tm=512):
    n = x2.shape[0]
    return pl.pallas_call(
        _inproj_kernel,
        grid=(n // tm,),
        in_specs=[pl.BlockSpec((tm, D_MODEL), lambda i: (i, 0)),
                  pl.BlockSpec((D_MODEL, IN_COLS_PAD), lambda i: (0, 0))],
        out_specs=pl.BlockSpec((tm, IN_COLS_PAD), lambda i: (i, 0)),
        out_shape=jax.ShapeDtypeStruct((n, IN_COLS_PAD), F32),
        compiler_params=_cparams(("parallel",)),
        name="inproj",
    )(x2, w_pad)


def _retention_kernel(q_ref, k_ref, v_ref, g_ref, cos_ref, sin_ref, dmat_ref, zeta_ref, xi_ref,
                      cd_ref, o_ref, s_ref):
    @pl.when(pl.program_id(1) == 0)
    def _():
        s_ref[...] = jnp.zeros_like(s_ref)

    cos = cos_ref[...]
    sin = sin_ref[...]

    def rope(t):
        t1, t2 = t[:, :LANE], t[:, LANE:]
        return jnp.concatenate([t1 * cos - t2 * sin, t1 * sin + t2 * cos], axis=1)

    qr = rope(q_ref[...])
    kr = rope(k_ref[...]) * (R_DK ** -0.5)
    v = v_ref[...]
    g = g_ref[...]
    s_prev = s_ref[...]
    qx = qr * xi_ref[...]
    kz = kr * zeta_ref[...]
    krb = kr.astype(BF16)
    vb = v.astype(BF16)
    lane = lax.broadcasted_iota(jnp.int32, qr.shape, 1)
    head_of_lane = (lane % LANE) // (R_DK // 2)
    for h in range(R_HEADS):
        mh = head_of_lane == h
        sl = slice(h * R_DV, (h + 1) * R_DV)
        scores = _dot_nt(jnp.where(mh, qr, 0.0), krb) * dmat_ref[h]
        o = _dot(scores, vb[:, sl]) + _dot(jnp.where(mh, qx, 0.0), s_prev[:, sl])
        mu = jnp.mean(o, axis=-1, keepdims=True)
        d = o - mu
        var = jnp.mean(d * d, axis=-1, keepdims=True)
        o_ref[:, sl] = d * lax.rsqrt(var + LN_EPS) * _silu(g[:, sl])
    s_ref[...] = s_prev * cd_ref[...] + _dot(kz.T, vb)


def _retention_tables(t_len):
    c = R_CHUNK
    half = R_DK // 2
    inv = ROPE_BASE ** (-jnp.arange(0, R_DK, 2, dtype=F32) / R_DK)
    pos = jnp.arange(t_len, dtype=F32)
    ang = pos[:, None] * inv[None, :]
    cos = jnp.tile(jnp.cos(ang), (1, R_HEADS))
    sin = jnp.tile(jnp.sin(ang), (1, R_HEADS))
    log_g = jnp.log(1.0 - 2.0 ** (-5.0 - jnp.arange(R_HEADS, dtype=F32)))
    idx = jnp.arange(c, dtype=F32)
    causal = jnp.tril(jnp.ones((c, c), dtype=bool))
    dmat = jnp.exp(jnp.where(causal, (idx[:, None] - idx[None, :])[None] * log_g[:, None, None], -jnp.inf))
    zeta = jnp.exp((c - 1.0 - idx)[None, :] * log_g[:, None])
    xi = jnp.exp((idx + 1.0)[None, :] * log_g[:, None])
    lane_head = (jnp.arange(2 * LANE) % LANE) // half
    zeta_t = zeta.T[:, lane_head]
    xi_t = xi.T[:, lane_head]
    cd = jnp.exp(c * log_g)[lane_head]
    cd_t = jnp.broadcast_to(cd[:, None], (2 * LANE, R_HEADS * R_DV))
    return cos, sin, dmat, zeta_t, xi_t, cd_t


def _retention(hproj, batch, t_len):
    c = R_CHUNK
    nch = t_len // c
    cos, sin, dmat, zeta_t, xi_t, cd_t = _retention_tables(t_len)
    row = lambda b, n: b * nch + n
    const2 = lambda b, n: (0, 0)
    return pl.pallas_call(
        _retention_kernel,
        grid=(batch, nch),
        in_specs=[pl.BlockSpec((c, 256), lambda b, n: (row(b, n), COL_RQ // 256)),
                  pl.BlockSpec((c, 256), lambda b, n: (row(b, n), COL_RK // 256)),
                  pl.BlockSpec((c, 512), lambda b, n: (row(b, n), COL_RV // 512)),
                  pl.BlockSpec((c, 512), lambda b, n: (row(b, n), COL_RG // 512)),
                  pl.BlockSpec((c, LANE), lambda b, n: (n, 0)),
                  pl.BlockSpec((c, LANE), lambda b, n: (n, 0)),
                  pl.BlockSpec((R_HEADS, c, c), lambda b, n: (0, 0, 0)),
                  pl.BlockSpec((c, 256), const2),
                  pl.BlockSpec((c, 256), const2),
                  pl.BlockSpec((256, R_HEADS * R_DV), const2)],
        out_specs=pl.BlockSpec((c, R_HEADS * R_DV), lambda b, n: (row(b, n), 0)),
        out_shape=jax.ShapeDtypeStruct((batch * t_len, R_HEADS * R_DV), F32),
        scratch_shapes=[pltpu.VMEM((256, R_HEADS * R_DV), F32)],
        compiler_params=_cparams(("parallel", "arbitrary")),
        name="retention",
    )(hproj, hproj, hproj, hproj, cos, sin, dmat, zeta_t, xi_t, cd_t)


def _x3(x):
    hi = x.astype(BF16)
    r1 = x - hi.astype(F32)
    mid = r1.astype(BF16)
    lo = (r1 - mid.astype(F32)).astype(BF16)
    return hi, mid, lo


def _unit_lower_inverses_minus_eye(a_list, ri, ci):
    c = a_list[0].shape[0]

    def same_block(m):
        return (ri // m) == (ci // m)

    base = 8
    p = [jnp.where(same_block(base), a, 0.0) for a in a_list]
    e = [-x for x in p]
    for _ in range(int(math.log2(base)) - 1):
        p = [_dot(x, x) for x in p]
        ep = [_dot(ei, pi) for ei, pi in zip(e, p)]
        e = [ei + pi + epi for ei, pi, epi in zip(e, p, ep)]
    m = base
    while m < c:
        off = same_block(2 * m) & jnp.logical_not(same_block(m))
        a_off = [jnp.where(off, a, 0.0) for a in a_list]
        x = [ao + _dot(ei, ao) for ei, ao in zip(e, a_off)]
        xe = [_dot(xi, ei) for xi, ei in zip(x, e)]
        e = [ei - xi - xei for ei, xi, xei in zip(e, x, xe)]
        m *= 2
    return e


def _gdn_kernel(qkv_ref, gg_ref, gab_ref, convw_ref, alog_ref, dtb_ref, nw_ref, o_ref,
                s_ref, carry_ref):
    cb = qkv_ref.shape[0]
    c = G_CHUNK

    @pl.when(pl.program_id(1) == 0)
    def _():
        s_ref[...] = jnp.zeros_like(s_ref)
        carry_ref[...] = jnp.zeros_like(carry_ref)

    x = qkv_ref[...]
    prev8 = carry_ref[...]
    row8 = lax.broadcasted_iota(jnp.int32, prev8.shape, 0)
    acc = x * convw_ref[CONV_K - 1:CONV_K, :]
    for j in range(1, CONV_K):
        rolled = pltpu.roll(x, j, 0)
        head8 = jnp.where(row8 < j, pltpu.roll(prev8, j, 0), rolled[:8])
        shifted = jnp.concatenate([head8, rolled[8:]], axis=0)
        acc = acc + shifted * convw_ref[CONV_K - 1 - j:CONV_K - j, :]
    carry_ref[...] = x[cb - 8:, :]
    qkv = _silu(acc)

    gab = gab_ref[...]
    z = gab + dtb_ref[...]
    softplus = jnp.maximum(z, 0.0) + jnp.log1p(jnp.exp(-jnp.abs(z)))
    g = -jnp.exp(alog_ref[...]) * softplus
    beta = 1.0 / (1.0 + jnp.exp(-gab))
    rb = lax.broadcasted_iota(jnp.int32, (cb, cb), 0)
    cbi = lax.broadcasted_iota(jnp.int32, (cb, cb), 1)
    tri = jnp.where((rb >= cbi) & ((rb // c) == (cbi // c)), 1.0, 0.0).astype(BF16)
    gc = sum(jnp.dot(tri, piece, preferred_element_type=F32) for piece in _x3(g))
    gc_t = gc.T

    ri = lax.broadcasted_iota(jnp.int32, (c, c), 0)
    ci = lax.broadcasted_iota(jnp.int32, (c, c), 1)
    causal = ri >= ci
    strict = ri > ci
    nw = nw_ref[...]
    gg = gg_ref[...]

    probs = [(j, h) for j in range(cb // c) for h in range(G_HEADS)]
    pre = []
    for j, h in probs:
        rows = slice(j * c, (j + 1) * c)
        q = qkv[rows, h * G_DK:(h + 1) * G_DK]
        k = qkv[rows, G_HEADS * G_DK + h * G_DK:G_HEADS * G_DK + (h + 1) * G_DK]
        v = qkv[rows, 2 * G_HEADS * G_DK + h * G_DV:2 * G_HEADS * G_DK + (h + 1) * G_DV]
        q = q * lax.rsqrt(jnp.sum(q * q, axis=-1, keepdims=True) + NORM_EPS) * (G_DK ** -0.5)
        k = k * lax.rsqrt(jnp.sum(k * k, axis=-1, keepdims=True) + NORM_EPS)
        gcol = gc[rows, h:h + 1]
        grow = gc_t[h:h + 1, rows]
        bcol = beta[rows, G_HEADS + h:G_HEADS + h + 1]
        decay = jnp.exp(jnp.where(causal, gcol - grow, -jnp.inf))
        egc = jnp.exp(gcol)
        kb = k * bcol
        glast = gcol[c - 1:c, :]
        pre.append(dict(q=q, k=k, kb=kb, decay=decay,
                        rhs=jnp.concatenate([v * bcol, kb * egc], axis=1),
                        q_dec=q * egc, k_dec_t=(k * jnp.exp(glast - gcol)).T,
                        s_decay=jnp.exp(glast)))
    kk = [_dot_nt(jnp.concatenate([p["kb"], p["q"]], axis=0), p["k"]) for p in pre]
    a_list = [jnp.where(strict, x[:c] * p["decay"], 0.0) for x, p in zip(kk, pre)]
    qk_list = [jnp.where(causal, x[c:] * p["decay"], 0.0) for x, p in zip(kk, pre)]
    e_list = _unit_lower_inverses_minus_eye(a_list, ri, ci)
    sol = [p["rhs"] + _dot(e, p["rhs"]) for e, p in zip(e_list, pre)]

    state = [s_ref[h] for h in range(G_HEADS)]
    for j in range(cb // c):
        rows = slice(j * c, (j + 1) * c)
        idx = [j * G_HEADS + h for h in range(G_HEADS)]
        wq = [_dot(jnp.concatenate([sol[i][:, G_DV:], pre[i]["q_dec"]], axis=0), state[h])
              for h, i in enumerate(idx)]
        v_new = [sol[i][:, :G_DV] - wq[h][:c] for h, i in enumerate(idx)]
        o_intra = [_dot(qk_list[i], v_new[h]) for h, i in enumerate(idx)]
        s_add = [_dot(pre[i]["k_dec_t"], v_new[h]) for h, i in enumerate(idx)]
        for h, i in enumerate(idx):
            sl = slice(h * G_DV, (h + 1) * G_DV)
            state[h] = state[h] * pre[i]["s_decay"] + s_add[h]
            o = wq[h][c:] + o_intra[h]
            o = o * lax.rsqrt(jnp.mean(o * o, axis=-1, keepdims=True) + NORM_EPS)
            o_ref[rows, sl] = o * nw * _silu(gg[rows, sl])

    for h in range(G_HEADS):
        s_ref[h] = state[h]


def _lane_pad(vec):
    return jnp.zeros((1, LANE), F32).at[0, :vec.shape[0]].set(vec.astype(F32))


def _gdn(hproj, conv_w, a_log, dt_bias, gdn_norm_w, batch, t_len):
    c = G_BLOCK
    nch = t_len // c
    convw = jnp.zeros((8, CONV_CH), F32).at[:CONV_K].set(conv_w.astype(F32))
    alog = _lane_pad(a_log)
    dtb = _lane_pad(dt_bias)
    nw = gdn_norm_w.astype(F32).reshape(1, G_DV)
    row = lambda b, n: b * nch + n
    const2 = lambda b, n: (0, 0)
    return pl.pallas_call(
        _gdn_kernel,
        grid=(batch, nch),
        in_specs=[pl.BlockSpec((c, CONV_CH), lambda b, n: (row(b, n), COL_GQKV // CONV_CH)),
                  pl.BlockSpec((c, 512), lambda b, n: (row(b, n), COL_GG // 512)),
                  pl.BlockSpec((c, LANE), lambda b, n: (row(b, n), COL_GAB // LANE)),
                  pl.BlockSpec((8, CONV_CH), const2),
                  pl.BlockSpec((1, LANE), const2),
                  pl.BlockSpec((1, LANE), const2),
                  pl.BlockSpec((1, G_DV), const2)],
        out_specs=pl.BlockSpec((c, G_HEADS * G_DV), lambda b, n: (row(b, n), 0)),
        out_shape=jax.ShapeDtypeStruct((batch * t_len, G_HEADS * G_DV), F32),
        scratch_shapes=[pltpu.VMEM((G_HEADS, G_DK, G_DV), F32),
                        pltpu.VMEM((8, CONV_CH), F32)],
        compiler_params=_cparams(("parallel", "arbitrary")),
        name="gdn",
    )(hproj, hproj, hproj, convw, alog, dtb, nw)


def _layer_norm(y, g, b):
    mu = jnp.mean(y, axis=-1, keepdims=True)
    d = y - mu
    var = jnp.mean(d * d, axis=-1, keepdims=True)
    return d * lax.rsqrt(var + LN_EPS) * g + b


def _outproj_ln_kernel(ro_ref, go_ref, x_ref, w_ref, g_ref, b_ref, o_ref):
    half = ro_ref.shape[1]
    mix = _dot(ro_ref[...], w_ref[:half, :]) + _dot(go_ref[...], w_ref[half:, :])
    o_ref[...] = _layer_norm(ALPHA * x_ref[...] + mix, g_ref[...], b_ref[...])


def _outproj_ln(ro, go, x2, w_o, ln_g, ln_b, tm=512):
    n = x2.shape[0]
    half = ro.shape[1]
    return pl.pallas_call(
        _outproj_ln_kernel,
        grid=(n // tm,),
        in_specs=[pl.BlockSpec((tm, half), lambda i: (i, 0)),
                  pl.BlockSpec((tm, half), lambda i: (i, 0)),
                  pl.BlockSpec((tm, D_MODEL), lambda i: (i, 0)),
                  pl.BlockSpec((2 * half, D_MODEL), lambda i: (0, 0)),
                  pl.BlockSpec((1, D_MODEL), lambda i: (0, 0)),
                  pl.BlockSpec((1, D_MODEL), lambda i: (0, 0))],
        out_specs=pl.BlockSpec((tm, D_MODEL), lambda i: (i, 0)),
        out_shape=jax.ShapeDtypeStruct((n, D_MODEL), F32),
        compiler_params=_cparams(("parallel",)),
        name="outproj_ln",
    )(ro, go, x2, w_o.astype(BF16), ln_g.reshape(1, -1).astype(F32), ln_b.reshape(1, -1).astype(F32))


def _router_kernel(x_ref, w_ref, o_ref):
    logits = _dot_hi(x_ref[...], w_ref[...])
    lane = lax.broadcasted_iota(jnp.int32, logits.shape, 1)
    neg = jnp.float32(-jnp.inf)
    lg = jnp.where(lane < N_EXPERTS, logits, neg)
    m1 = jnp.max(lg, axis=-1, keepdims=True)
    i1 = jnp.min(jnp.where(lg == m1, lane, LANE), axis=-1, keepdims=True)
    lg2 = jnp.where(lane == i1, neg, lg)
    m2 = jnp.max(lg2, axis=-1, keepdims=True)
    i2 = jnp.min(jnp.where(lg2 == m2, lane, LANE), axis=-1, keepdims=True)
    e2 = jnp.exp(m2 - m1)
    den = 1.0 + e2
    o_ref[...] = jnp.where(lane == i1, 1.0 / den, 0.0) + jnp.where(lane == i2, e2 / den, 0.0)


def _router(x2, router_w, tm=512):
    n = x2.shape[0]
    w = jnp.zeros((D_MODEL, LANE), F32).at[:, :N_EXPERTS].set(router_w.astype(F32))
    return pl.pallas_call(
        _router_kernel,
        grid=(n // tm,),
        in_specs=[pl.BlockSpec((tm, D_MODEL), lambda i: (i, 0)),
                  pl.BlockSpec((D_MODEL, LANE), lambda i: (0, 0))],
        out_specs=pl.BlockSpec((tm, LANE), lambda i: (i, 0)),
        out_shape=jax.ShapeDtypeStruct((n, LANE), F32),
        compiler_params=_cparams(("parallel",)),
        name="router",
    )(x2, w)


def _ffn_ln_kernel(x_ref, comb_ref, wg_ref, wu_ref, wd_ref, g_ref, b_ref, o_ref, acc_ref, *,
                   weighted):
    e = pl.program_id(1)
    f = pl.program_id(2)

    @pl.when((e == 0) & (f == 0))
    def _():
        acc_ref[...] = jnp.zeros_like(acc_ref)

    xb = x_ref[...].astype(BF16)
    hg = jnp.dot(xb, wg_ref[0], preferred_element_type=F32)
    hu = jnp.dot(xb, wu_ref[0], preferred_element_type=F32)
    hmid = _silu(hg) * hu
    if weighted:
        comb = comb_ref[...]
        lane = lax.broadcasted_iota(jnp.int32, comb.shape, 1)
        hmid = hmid * jnp.sum(jnp.where(lane == e, comb, 0.0), axis=-1, keepdims=True)
    acc_ref[...] += jnp.dot(hmid.astype(BF16), wd_ref[0], preferred_element_type=F32)

    @pl.when((e == pl.num_programs(1) - 1) & (f == pl.num_programs(2) - 1))
    def _():
        o_ref[...] = _layer_norm(ALPHA * x_ref[...] + acc_ref[...], g_ref[...], b_ref[...])


def _ffn_ln(x2, comb, w_gate, w_up, w_down, ln_g, ln_b, *, weighted, tm, tf):
    n = x2.shape[0]
    ne, _, dff = w_gate.shape
    return pl.pallas_call(
        functools.partial(_ffn_ln_kernel, weighted=weighted),
        grid=(n // tm, ne, dff // tf),
        in_specs=[pl.BlockSpec((tm, D_MODEL), lambda i, e, f: (i, 0)),
                  pl.BlockSpec((tm, LANE), lambda i, e, f: (i, 0)),
                  pl.BlockSpec((1, D_MODEL, tf), lambda i, e, f: (e, 0, f)),
                  pl.BlockSpec((1, D_MODEL, tf), lambda i, e, f: (e, 0, f)),
                  pl.BlockSpec((1, tf, D_MODEL), lambda i, e, f: (e, f, 0)),
                  pl.BlockSpec((1, D_MODEL), lambda i, e, f: (0, 0)),
                  pl.BlockSpec((1, D_MODEL), lambda i, e, f: (0, 0))],
        out_specs=pl.BlockSpec((tm, D_MODEL), lambda i, e, f: (i, 0)),
        out_shape=jax.ShapeDtypeStruct((n, D_MODEL), F32),
        scratch_shapes=[pltpu.VMEM((tm, D_MODEL), F32)],
        compiler_params=_cparams(("parallel", "arbitrary", "arbitrary")),
        name="moe_ln" if weighted else "ffn_ln",
    )(x2, comb, w_gate.astype(BF16), w_up.astype(BF16), w_down.astype(BF16),
      ln_g.reshape(1, -1).astype(F32), ln_b.reshape(1, -1).astype(F32))


def _prep_w_in(w):
    half = R_DK // 2
    lane = np.arange(LANE)
    first = (lane // half) * R_DK + lane % half
    perm = np.concatenate([first, first + half])
    cols = np.arange(IN_COLS)
    cols[COL_RQ:COL_RQ + 256] = COL_RQ + perm
    cols[COL_RK:COL_RK + 256] = COL_RK + perm
    w = w[:, cols]
    return jnp.pad(w, ((0, 0), (0, IN_COLS_PAD - IN_COLS))).astype(BF16)


def kernel(x, w_in, conv_w, a_log, dt_bias, gdn_norm_w, w_o, ln1_g, ln1_b, ln2_g, ln2_b,
           ffn_w_gate, ffn_w_up, ffn_w_down, router_w, moe_w_gate, moe_w_up, moe_w_down):
    batch, t_len, d = x.shape
    x2 = x.reshape(batch * t_len, d).astype(F32)
    for l in range(DEPTH):
        hproj = _inproj(x2, _prep_w_in(w_in[l]))
        ro = _retention(hproj, batch, t_len)
        go = _gdn(hproj, conv_w[l], a_log[l], dt_bias[l], gdn_norm_w[l], batch, t_len)
        x2 = _outproj_ln(ro, go, x2, w_o[l], ln1_g[l], ln1_b[l])
        if l % 2 == 0:
            ones = jnp.ones((x2.shape[0], LANE), F32)
            x2 = _ffn_ln(x2, ones, ffn_w_gate[l // 2][None], ffn_w_up[l // 2][None],
                         ffn_w_down[l // 2][None], ln2_g[l], ln2_b[l], weighted=False,
                         tm=512, tf=1408)
        else:
            comb = _router(x2, router_w[l // 2])
            x2 = _ffn_ln(x2, comb, moe_w_gate[l // 2], moe_w_up[l // 2], moe_w_down[l // 2],
                         ln2_g[l], ln2_b[l], weighted=True, tm=512, tf=896)
    return x2.reshape(batch, t_len, d).astype(x.dtype)
```
